```python
import jax, jax.numpy as jnp
from jax import lax
import numpy as np

D_MODEL = 2048
BATCH = 16
SEQ = 256
DEPTH = 4
DEC_BATCH = 4
DEC_SEQ = 4096
PAST_LEN = 512

GRID_W = 64
EPS = 1e-6
SC_W = 512
SC_K = 3
SG_W = 512
SG_CHUNK = 128
SG_GROUPS = 4
SG_GDIM = SG_W // SG_GROUPS
HG_HEADS = 4
HG_DK = 128
HG_DV = 128
HG_CHUNK = 64
MLA_HEADS = 8
MLA_NOPE = 128
MLA_ROPE = 64
MLA_QK = MLA_NOPE + MLA_ROPE
MLA_V = 128
MLA_Q_RANK = 512
MLA_KV_RANK = 512
ROPE_BASE = 10000.0
ATTN_BLOCK = 128
N_BRANCH = 4
A_W = SC_W
B_W = SG_W
C_W = HG_HEADS * HG_DV
D_W = MLA_HEADS * MLA_V
MIX_W = A_W + B_W + C_W + D_W
IN_SPLITS = (SC_W, SC_W, SC_W, SG_W, SG_W, HG_HEADS * HG_DK, HG_HEADS * HG_DV, HG_HEADS * HG_DK,
             HG_HEADS * HG_DK, HG_HEADS * HG_DV, MLA_Q_RANK, MLA_KV_RANK, MLA_ROPE, N_BRANCH * D_MODEL)
IN_COLS = sum(IN_SPLITS)
D_FF = 5632
N_EXPERTS = 8
TOP_K = 2
EXP_FF = 5632
N_DENSE = (DEPTH + 1) // 2
N_MOE = DEPTH // 2

kernel_name = "hybrid_flow_prefix_conv_sgu_hgrn2_mla_moe_step"

F32 = jnp.float32


def rms_norm(x, gain):
    xf = x.astype(F32)
    y = xf * lax.rsqrt(jnp.mean(xf * xf, axis=-1, keepdims=True) + EPS)
    return (y * gain.astype(F32)).astype(x.dtype)


def layer_norm_plain(x):
    xf = x.astype(F32)
    mu = jnp.mean(xf, axis=-1, keepdims=True)
    var = jnp.mean(jnp.square(xf - mu), axis=-1, keepdims=True)
    return ((xf - mu) * lax.rsqrt(var + EPS)).astype(x.dtype)


def modulation(cvec, w_mod_l, b_mod_l):
    m = jax.nn.silu(cvec) @ w_mod_l + b_mod_l
    return jnp.split(m[:, None, :], 6, axis=-1)


def short_conv(x, w):
    ch = x.shape[-1]
    return lax.conv_general_dilated(x, w[:, None, :].astype(x.dtype), window_strides=(1,),
                                    padding=((SC_K // 2, SC_K // 2),),
                                    dimension_numbers=('NWC', 'WIO', 'NWC'),
                                    feature_group_count=ch)


def spatial_gate(u, v, w_s, b_s):
    bn, t, _ = v.shape
    n = t // SG_CHUNK
    vr = layer_norm_plain(v).reshape(bn, n, SG_CHUNK, SG_GROUPS, SG_GDIM)
    z = jnp.einsum('gts,bnsgc->bntgc', w_s, vr) + jnp.swapaxes(b_s, 0, 1)[None, None, :, :, None]
    return u * z.reshape(bn, t, SG_W)


def axial_rope(x):
    t = x.shape[1]
    rows = t // GRID_W
    row = jnp.repeat(jnp.arange(rows), GRID_W)
    col = jnp.tile(jnp.arange(GRID_W), rows)
    half = MLA_ROPE // 2
    inv = ROPE_BASE ** (-jnp.arange(0, half, 2, dtype=F32) / half)

    def rot(xa, pos):
        ang = pos.astype(F32)[:, None] * inv[None, :]
        cos = jnp.cos(ang)[None, :, None, :]
        sin = jnp.sin(ang)[None, :, None, :]
        x1, x2 = jnp.split(xa.astype(F32), 2, axis=-1)
        return jnp.concatenate([x1 * cos - x2 * sin, x1 * sin + x2 * cos], axis=-1)

    out = jnp.concatenate([rot(x[..., :half], row), rot(x[..., half:], col)], axis=-1)
    return out.astype(x.dtype)


def rope_part(x):
    return jnp.concatenate([x[..., :MLA_NOPE], axial_rope(x[..., MLA_NOPE:])], axis=-1)


def block_attention(q, k, v):
    bn, tq, h, dh = q.shape
    nb = tq // ATTN_BLOCK
    scale = dh ** -0.5
    qb = q.reshape(bn, nb, ATTN_BLOCK, h, dh).transpose(1, 0, 2, 3, 4)

    def one(qblk):
        s = jnp.einsum('bqhd,bkhd->bhqk', qblk, k, preferred_element_type=F32) * scale
        p = jax.nn.softmax(s, axis=-1)
        return jnp.einsum('bhqk,bkhv->bqhv', p.astype(v.dtype), v)

    o = lax.map(one, qb)
    return o.transpose(1, 0, 2, 3, 4).reshape(bn, tq, h * v.shape[-1])


def expand_kv(ckv, krope, w_ukv, kn_g):
    bn, t, _ = ckv.shape
    kv = (ckv @ w_ukv).reshape(bn, t, MLA_HEADS, MLA_NOPE + MLA_V)
    k_nope, v = kv[..., :MLA_NOPE], kv[..., MLA_NOPE:]
    k = jnp.concatenate([k_nope, jnp.broadcast_to(krope[:, :, None, :], (bn, t, MLA_HEADS, MLA_ROPE))], axis=-1)
    return rms_norm(k, kn_g), v


def mla_mixer(q_down, kv_down, k_rope, q_norm_g, kv_norm_g, w_uq, w_ukv, qn_g, kn_g,
              latent, ctx_ckv, ctx_krope):
    bn, t, _ = q_down.shape
    q = (rms_norm(q_down, q_norm_g) @ w_uq).reshape(bn, t, MLA_HEADS, MLA_QK)
    q = rms_norm(q, qn_g)
    ckv = rms_norm(kv_down, kv_norm_g)
    k, v = expand_kv(ckv, k_rope, w_ukv, kn_g)
    if latent:
        q = rope_part(q)
        k = rope_part(k)
        k_c, v_c = expand_kv(ctx_ckv, ctx_krope, w_ukv, kn_g)
        k = jnp.concatenate([k, k_c], axis=1)
        v = jnp.concatenate([v, v_c], axis=1)
    return block_attention(q, k, v), ckv


def gla_chunk_scan(q, k, v, logf, s0):
    bn, t, h, _ = q.shape
    n = t // HG_CHUNK

    def to_chunks(a):
        return a.reshape(bn, n, HG_CHUNK, h, a.shape[-1]).transpose(1, 0, 3, 2, 4)

    mask = jnp.tril(jnp.ones((HG_CHUNK, HG_CHUNK), dtype=bool))

    def step(s_prev, inp):
        qc, kc, vc, lc = inp
        lcum = jnp.cumsum(lc, axis=2)
        o_inter = jnp.einsum('bhtk,bhkv->bhtv', qc * jnp.exp(lcum), s_prev)
        diff = jnp.where(mask[:, :, None], lcum[:, :, :, None, :] - lcum[:, :, None, :, :], -jnp.inf)
        att = jnp.einsum('bhtk,bhsk,bhtsk->bhts', qc, kc, jnp.exp(diff))
        o = o_inter + jnp.einsum('bhts,bhsv->bhtv', att, vc)
        l_end = lcum[:, :, -1:, :]
        s_new = jnp.exp(l_end[:, :, 0, :])[..., None] * s_prev + jnp.einsum(
            'bhsk,bhsv->bhkv', kc * jnp.exp(l_end - lcum), vc)
        return s_new, o

    s_fin, o = lax.scan(step, s0, (to_chunks(q), to_chunks(k), to_chunks(v), to_chunks(logf)))
    o = o.transpose(1, 0, 3, 2, 4).reshape(bn, t, h, v.shape[-1])
    return o, s_fin


def hgrn2_mixer(q_raw, i_raw, ff_raw, fb_raw, g_raw, lb_f, lb_b, norm_gain, s0):
    bn, t, _ = q_raw.shape

    def heads(a, d):
        return a.astype(F32).reshape(bn, t, HG_HEADS, d)

    q = jax.nn.silu(heads(q_raw, HG_DK))
    i = heads(i_raw, HG_DV)

    def forget(fr, lb):
        z = heads(fr, HG_DK)
        lbh = lb.astype(F32).reshape(HG_HEADS, HG_DK)
        logf = jnp.logaddexp(jnp.log(lbh), jnp.log1p(-lbh) + jax.nn.log_sigmoid(z))
        kk = (1.0 - lbh) * jax.nn.sigmoid(-z)
        return logf, kk

    logf_f, k_f = forget(ff_raw, lb_f)
    logf_b, k_b = forget(fb_raw, lb_b)
    s0 = s0.astype(F32)
    o_f, s_f = gla_chunk_scan(q, k_f, i, logf_f, s0[:, 0])
    flip = lambda a: jnp.flip(a, axis=1)
    o_b, s_b = gla_chunk_scan(flip(q), flip(k_b), flip(i), flip(logf_b), s0[:, 1])
    o = rms_norm(o_f + flip(o_b), norm_gain)
    o = o.reshape(bn, t, HG_HEADS * HG_DV) * jax.nn.silu(g_raw.astype(F32))
    return o.astype(q_raw.dtype), jnp.stack([s_f, s_b], axis=1)


def swiglu(h, wg, wu, wd):
    return (jax.nn.silu(h @ wg) * (h @ wu)) @ wd


def moe_ffn(h, w_router, b_router, wg, wu, wd):
    logits = (h @ w_router).astype(F32) + b_router.astype(F32)
    top_v, top_i = lax.top_k(logits, TOP_K)
    w = jax.nn.softmax(top_v, axis=-1)
    combine = jnp.sum(jax.nn.one_hot(top_i, N_EXPERTS, dtype=F32) * w[..., None], axis=-2).astype(h.dtype)
    out = jnp.zeros_like(h)
    for e in range(N_EXPERTS):
        out = out + combine[..., e:e + 1] * swiglu(h, wg[e], wu[e], wd[e])
    return out


def trunk_layer(x, cvec, l, latent, ctx_ckv, ctx_krope, s0, lb_f, lb_b, P):
    bn, t, _ = x.shape
    sh1, sc1, g1, sh2, sc2, g2 = modulation(cvec, P['w_mod'][l], P['b_mod'][l])
    h = rms_norm(x, P['norm1'][l]) * (1 + sc1) + sh1
    offs = np.cumsum(IN_SPLITS)[:-1].tolist()
    (a_h, a_b, a_c, s_u, s_v, hq, hi, hff, hfb, hg, m_q, m_kv, m_kr, gates) = jnp.split(
        h @ P['w_in'][l], offs, axis=-1)
    y_a = a_b * short_conv(a_c * a_h, P['conv_w'][l])
    y_b = spatial_gate(s_u, s_v, P['sg_w'][l], P['sg_b'][l])
    y_c, hg_state = hgrn2_mixer(hq, hi, hff, hfb, hg, lb_f, lb_b, P['hg_norm'][l], s0)
    y_d, ckv = mla_mixer(m_q, m_kv, m_kr, P['mla_q_norm'][l], P['mla_kv_norm'][l], P['w_uq'][l],
                         P['w_ukv'][l], P['qk_norm_q'][l], P['qk_norm_k'][l], latent, ctx_ckv, ctx_krope)
    wb = P['w_branch'][l]
    gs = jax.nn.sigmoid(gates.reshape(bn, t, N_BRANCH, D_MODEL))
    merged = (gs[:, :, 0] * (y_a @ wb[:A_W])
              + gs[:, :, 1] * (y_b @ wb[A_W:A_W + B_W])
              + gs[:, :, 2] * (y_c @ wb[A_W + B_W:A_W + B_W + C_W])
              + gs[:, :, 3] * (y_d @ wb[A_W + B_W + C_W:]))
    x = x + g1 * (merged @ P['w_o'][l])
    h2 = rms_norm(x, P['norm2'][l]) * (1 + sc2) + sh2
    if l % 2 == 0:
        j = l // 2
        f = swiglu(h2, P['w_ff_gate'][j], P['w_ff_up'][j], P['w_ff_down'][j])
    else:
        j = l // 2
        f = moe_ffn(h2, P['w_router'][j], P['b_router'][j], P['w_moe_gate'][j], P['w_moe_up'][j],
                    P['w_moe_down'][j])
    x = x + g2 * f
    return x, ckv, m_kr, hg_state


def setup_inputs(seed: int = 0) -> dict:
    key = jax.random.key(seed)
    ks = iter(jax.random.split(key, 48))

    def nrm(shape, scale=1.0):
        return jax.random.normal(next(ks), shape, F32) * scale

    def gain(shape):
        return 1.0 + nrm(shape, 0.1)

    D = D_MODEL
    return {
        'x_prompt': nrm((BATCH, SEQ, D)),
        'x_sample': nrm((DEC_BATCH, DEC_SEQ, D)),
        'c': nrm((DEC_BATCH, D)),
        'cache_mla_ckv': nrm((DEC_BATCH, DEPTH, PAST_LEN, MLA_KV_RANK)),
        'cache_mla_krope': nrm((DEC_BATCH, DEPTH, PAST_LEN, MLA_ROPE)),
        'state_hgrn': nrm((DEC_BATCH, DEPTH, 2, HG_HEADS, HG_DK, HG_DV), 0.5),
        'c_ctx': nrm((D,)),
        'norm1': gain((DEPTH, D)),
        'norm2': gain((DEPTH, D)),
        'w_mod': nrm((DEPTH, D, 6 * D), 0.5 * D ** -0.5),
        'b_mod': nrm((DEPTH, 6 * D), 0.02),
        'w_in': nrm((DEPTH, D, IN_COLS), D ** -0.5),
        'conv_w': nrm((DEPTH, SC_K, SC_W), SC_K ** -0.5),
        'sg_w': nrm((DEPTH, SG_GROUPS, SG_CHUNK, SG_CHUNK), SG_CHUNK ** -0.5),
        'sg_b': gain((DEPTH, SG_GROUPS, SG_CHUNK)),
        'hg_lb': nrm((2, DEPTH, HG_HEADS * HG_DK), 0.5),
        'hg_norm': gain((DEPTH, HG_DV)),
        'mla_q_norm': gain((DEPTH, MLA_Q_RANK)),
        'mla_kv_norm': gain((DEPTH, MLA_KV_RANK)),
        'w_uq': nrm((DEPTH, MLA_Q_RANK, MLA_HEADS * MLA_QK), MLA_Q_RANK ** -0.5),
        'w_ukv': nrm((DEPTH, MLA_KV_RANK, MLA_HEADS * (MLA_NOPE + MLA_V)), MLA_KV_RANK ** -0.5),
        'qk_norm_q': gain((DEPTH, MLA_QK)),
        'qk_norm_k': gain((DEPTH, MLA_QK)),
        'w_branch': nrm((DEPTH, MIX_W, D), SC_W ** -0.5),
        'w_o': nrm((DEPTH, D, D), D ** -0.5),
        'w_ff_gate': nrm((N_DENSE, D, D_FF), D ** -0.5),
        'w_ff_up': nrm((N_DENSE, D, D_FF), D ** -0.5),
        'w_ff_down': nrm((N_DENSE, D_FF, D), D_FF ** -0.5),
        'w_router': nrm((N_MOE, D, N_EXPERTS), D ** -0.5),
        'b_router': nrm((N_MOE, N_EXPERTS), 0.01),
        'w_moe_gate': nrm((N_MOE, N_EXPERTS, D, EXP_FF), D ** -0.5),
        'w_moe_up': nrm((N_MOE, N_EXPERTS, D, EXP_FF), D ** -0.5),
        'w_moe_down': nrm((N_MOE, N_EXPERTS, EXP_FF, D), EXP_FF ** -0.5),
    }


def reference(x_prompt, x_sample, c, cache_mla_ckv, cache_mla_krope, state_hgrn, c_ctx,
              norm1, norm2, w_mod, b_mod, w_in, conv_w, sg_w, sg_b, hg_lb, hg_norm,
              mla_q_norm, mla_kv_norm, w_uq, w_ukv, qk_norm_q, qk_norm_k, w_branch, w_o,
              w_ff_gate, w_ff_up, w_ff_down, w_router, b_router, w_moe_gate, w_moe_up, w_moe_down):
    P = {'norm1': norm1, 'norm2': norm2, 'w_mod': w_mod, 'b_mod': b_mod, 'w_in': w_in,
         'conv_w': conv_w, 'sg_w': sg_w, 'sg_b': sg_b, 'hg_norm': hg_norm,
         'mla_q_norm': mla_q_norm, 'mla_kv_norm': mla_kv_norm, 'w_uq': w_uq, 'w_ukv': w_ukv,
         'qk_norm_q': qk_norm_q, 'qk_norm_k': qk_norm_k, 'w_branch': w_branch, 'w_o': w_o,
         'w_ff_gate': w_ff_gate, 'w_ff_up': w_ff_up, 'w_ff_down': w_ff_down,
         'w_router': w_router, 'b_router': b_router, 'w_moe_gate': w_moe_gate,
         'w_moe_up': w_moe_up, 'w_moe_down': w_moe_down}
    lb_cum = jnp.cumsum(jax.nn.softmax(hg_lb.astype(F32), axis=1), axis=1)
    lb_all = lb_cum - lb_cum[:, :1]
    xp = x_prompt
    xs = x_sample
    ckv_list, kr_list, st_list = [], [], []
    zero_state = jnp.zeros((xp.shape[0], 2, HG_HEADS, HG_DK, HG_DV), F32)
    for l in range(DEPTH):
        xp, ckv, kr, st = trunk_layer(xp, c_ctx[None, :], l, False, None, None, zero_state,
                                      lb_all[0, l], lb_all[1, l], P)
        ckv_list.append(ckv)
        kr_list.append(kr)
        st_list.append(st)
        xs, _, _, _ = trunk_layer(xs, c, l, True, cache_mla_ckv[:, l], cache_mla_krope[:, l],
                                  state_hgrn[:, l], lb_all[0, l], lb_all[1, l], P)
    new_mla_ckv = jnp.stack(ckv_list, axis=1)
    new_mla_krope = jnp.stack(kr_list, axis=1)
    new_hgrn_state = jnp.stack(st_list, axis=1)
    return (xp, xs, new_mla_ckv, new_mla_krope, new_hgrn_state)
```

```python
import functools
import math

import numpy as np
import jax
import jax.numpy as jnp
from jax import lax
from jax.experimental import pallas as pl
from jax.experimental.pallas import tpu as pltpu

F32 = jnp.float32
BF16 = jnp.bfloat16

D_MODEL = 2048
DEPTH = 4
GRID_W = 64
EPS = 1e-6
SC_W = 512
SG_W = 512
SG_CHUNK = 128
SG_GROUPS = 4
HG_HEADS = 4
HG_DK = 128
HG_DV = 128
HG_W = HG_HEADS * HG_DK
MLA_HEADS = 8
MLA_NOPE = 128
MLA_ROPE = 64
MLA_QK = MLA_NOPE + MLA_ROPE
MLA_V = 128
MLA_Q_RANK = 512
MLA_KV_RANK = 512
ROPE_BASE = 10000.0
N_BRANCH = 4
MIX_W = 2560
D_FF = 5632
N_EXPERTS = 8

MIXER_COLS = 6144
GATE_COL0 = MIXER_COLS
PROJ_COLS = MIXER_COLS + N_BRANCH * D_MODEL
COL_AH, COL_AB, COL_AC, COL_SU, COL_SV, COL_HQ, COL_HI, COL_HFF, COL_HFB, COL_HG, COL_MQ, COL_MKV = (
    512 * n for n in range(12))
KR_COLS = 256
QK_SLAB = 256

V7X_VMEM_LIMIT = 56 * 1024 * 1024
MOD_GROUP_ROWS = 4096
HG_CHUNK = 128
HG_LEVELS = 7


def _cparams(sem):
    return pltpu.CompilerParams(dimension_semantics=sem, vmem_limit_bytes=V7X_VMEM_LIMIT)


def _dot(a, b):
    return jnp.dot(a, b, preferred_element_type=F32)


def _dot_nt(a, b):
    return lax.dot_general(a, b, (((1,), (1,)), ((), ())), preferred_element_type=F32)


def _dot_tn(a, b):
    return lax.dot_general(a, b, (((0,), (0,)), ((), ())), preferred_element_type=F32)


def _sigmoid(x):
    return 1.0 / (1.0 + jnp.exp(-x))


def _silu(x):
    return x * _sigmoid(x)


def _mod_kernel(cv_ref, w_ref, b_ref, o_ref):
    s = _silu(cv_ref[...]).astype(BF16)
    o_ref[0] = _dot(s, w_ref[0].astype(BF16)) + b_ref[0]


def modulation_all(cv, w_mod, b_mod, tn=1024):
    depth, d, n = w_mod.shape
    return pl.pallas_call(
        _mod_kernel,
        out_shape=jax.ShapeDtypeStruct((depth, 8, n), F32),
        grid=(depth, n // tn),
        in_specs=[pl.BlockSpec((8, d), lambda l, j: (0, 0)),
                  pl.BlockSpec((1, d, tn), lambda l, j: (l, 0, j)),
                  pl.BlockSpec((1, 1, tn), lambda l, j: (l, 0, j))],
        out_specs=pl.BlockSpec((1, 8, tn), lambda l, j: (l, 0, j)),
        compiler_params=_cparams(("parallel", "parallel")),
        name="modulation",
    )(cv, w_mod, b_mod.reshape(depth, 1, n))


def _mod_norm(x, gain, shift, scale):
    ms = jnp.mean(x * x, axis=-1, keepdims=True)
    y = x * lax.rsqrt(ms + EPS) * gain
    return y * (1.0 + scale) + shift


def _group_of(i, tm):
    return (i * tm) // MOD_GROUP_ROWS


def _proj_kernel(x_ref, mod_ref, gain_ref, w_ref, wkr_ref, o_ref, kr_ref, h_scr):
    @pl.when(pl.program_id(1) == 0)
    def _():
        m = mod_ref[0]
        h = _mod_norm(x_ref[...], gain_ref[...], m[0:1], m[1:2]).astype(BF16)
        h_scr[...] = h
        kr_ref[...] = _dot(h, wkr_ref[...])

    o_ref[...] = _dot(h_scr[...], w_ref[...]).astype(o_ref.dtype)


def input_projection(x, mod_l, gain, w_main, w_kr, tm=1024, tn=1024):
    nt, d = x.shape
    n = w_main.shape[1]
    return pl.pallas_call(
        _proj_kernel,
        out_shape=(jax.ShapeDtypeStruct((nt, n), F32), jax.ShapeDtypeStruct((nt, KR_COLS), F32)),
        grid=(nt // tm, n // tn),
        in_specs=[pl.BlockSpec((tm, d), lambda i, j: (i, 0)),
                  pl.BlockSpec((1, 6, d), lambda i, j: (_group_of(i, tm), 0, 0)),
                  pl.BlockSpec((1, d), lambda i, j: (0, 0)),
                  pl.BlockSpec((d, tn), lambda i, j: (0, j)),
                  pl.BlockSpec((d, KR_COLS), lambda i, j: (0, 0))],
        out_specs=(pl.BlockSpec((tm, tn), lambda i, j: (i, j)),
                   pl.BlockSpec((tm, KR_COLS), lambda i, j: (i, 0))),
        scratch_shapes=[pltpu.VMEM((tm, d), BF16)],
        compiler_params=_cparams(("parallel", "arbitrary")),
        name="input_projection",
    )(x, mod_l, gain, w_main, w_kr)


def _conv_kernel(ah, ab, ac, ahp, acp, ahn, acn, w_ref, o_ref, *, tm, n_ctx_rows, ctx_len, lat_len):
    r0 = pl.program_id(0) * tm
    in_ctx = r0 < n_ctx_rows
    pos = jnp.where(in_ctx, r0 % ctx_len, (r0 - n_ctx_rows) % lat_len)
    seq_len = jnp.where(in_ctx, ctx_len, lat_len)
    has_prev = (pos != 0).astype(F32)
    has_next = (pos + tm != seq_len).astype(F32)
    u = ac[...] * ah[...]
    up = acp[7:8, :] * ahp[7:8, :] * has_prev
    un = acn[0:1, :] * ahn[0:1, :] * has_next
    row = lax.broadcasted_iota(jnp.int32, u.shape, 0)
    u_prev = jnp.where(row == 0, up, pltpu.roll(u, 1, 0))
    u_next = jnp.where(row == tm - 1, un, pltpu.roll(u, tm - 1, 0))
    w = w_ref[...]
    o_ref[...] = (ab[...] * (w[0:1] * u_prev + w[1:2] * u + w[2:3] * u_next)).astype(o_ref.dtype)


def short_conv_mixer(p, conv_w_l, n_ctx_rows, ctx_len, lat_len, tm=256):
    nt = p.shape[0]
    w = SC_W
    hb = tm // 8
    last8 = nt // 8 - 1
    cur = lambda c: pl.BlockSpec((tm, w), lambda i: (i, c))
    prv = lambda c: pl.BlockSpec((8, w), lambda i: (jnp.maximum(i * hb - 1, 0), c))
    nxt = lambda c: pl.BlockSpec((8, w), lambda i: (jnp.minimum((i + 1) * hb, last8), c))
    kern = functools.partial(_conv_kernel, tm=tm, n_ctx_rows=n_ctx_rows, ctx_len=ctx_len, lat_len=lat_len)
    return pl.pallas_call(
        kern,
        out_shape=jax.ShapeDtypeStruct((nt, w), BF16),
        grid=(nt // tm,),
        in_specs=[cur(COL_AH // w), cur(COL_AB // w), cur(COL_AC // w),
                  prv(COL_AH // w), prv(COL_AC // w), nxt(COL_AH // w), nxt(COL_AC // w),
                  pl.BlockSpec((3, w), lambda i: (0, 0))],
        out_specs=pl.BlockSpec((tm, w), lambda i: (i, 0)),
        compiler_params=_cparams(("parallel",)),
        name="short_conv",
    )(p, p, p, p, p, p, p, conv_w_l)


def _sgu_kernel(u_ref, v_ref, w_ref, b_ref, o_ref, *, nchunk):
    v = v_ref[...]
    mu = jnp.mean(v, axis=-1, keepdims=True)
    vc = v - mu
    var = jnp.mean(vc * vc, axis=-1, keepdims=True)
    vn = (vc * lax.rsqrt(var + EPS)).astype(BF16)
    gd = SG_W // SG_GROUPS
    for c in range(nchunk):
        rows = slice(c * SG_CHUNK, (c + 1) * SG_CHUNK)
        for g in range(SG_GROUPS):
            cols = slice(g * gd, (g + 1) * gd)
            z = _dot(w_ref[g], vn[rows, cols]) + b_ref[:, g:g + 1]
            o_ref[rows, cols] = (u_ref[rows, cols] * z).astype(o_ref.dtype)


def spatial_gate_mixer(p, sg_w_l, sg_b_l, tm=256):
    nt = p.shape[0]
    kern = functools.partial(_sgu_kernel, nchunk=tm // SG_CHUNK)
    return pl.pallas_call(
        kern,
        out_shape=jax.ShapeDtypeStruct((nt, SG_W), BF16),
        grid=(nt // tm,),
        in_specs=[pl.BlockSpec((tm, SG_W), lambda i: (i, COL_SU // SG_W)),
                  pl.BlockSpec((tm, SG_W), lambda i: (i, COL_SV // SG_W)),
                  pl.BlockSpec((SG_GROUPS, SG_CHUNK, SG_CHUNK), lambda i: (0, 0, 0)),
                  pl.BlockSpec((SG_CHUNK, SG_GROUPS), lambda i: (0, 0))],
        out_specs=pl.BlockSpec((tm, SG_W), lambda i: (i, 0)),
        compiler_params=_cparams(("parallel",)),
        name="spatial_gate",
    )(p, p, sg_w_l.astype(BF16), sg_b_l.T)


def _hgrn_constants():
    c = HG_CHUNK
    t = np.arange(c)[:, None]
    r = np.arange(c)[None, :]
    mats, masks = [], []
    incl = (r <= t)
    mats += [incl, ~incl]
    masks.append(t == r)
    for lv in range(HG_LEVELS):
        b = c >> (lv + 1)
        bnd = (t // (2 * b)) * (2 * b) + b - 1
        lo = np.minimum(t, bnd)
        hi = np.maximum(t, bnd)
        mats.append((r > lo) & (r <= hi))
        masks.append((t // (2 * b) == r // (2 * b)) & (t % (2 * b) >= b) & (r % (2 * b) < b))
    fwd_a = np.concatenate(mats, axis=0).astype(np.float32)
    fwd_m = np.stack(masks).astype(np.float32)
    bwd_a = np.concatenate([m[::-1, ::-1] for m in mats], axis=0).astype(np.float32)
    bwd_m = np.stack([m.T for m in masks]).astype(np.float32)
    return np.stack([fwd_a, bwd_a]), np.stack([fwd_m, bwd_m])


def _hgrn_kernel(qf_ref, if_ref, ff_ref, qb_ref, ib_ref, fb_ref, lb_ref, s0_ref, a_ref, m_ref,
                 of_ref, ob_ref, st_ref, s_scr):
    n = pl.program_id(1)
    c = HG_CHUNK

    @pl.when(n == 0)
    def _():
        s_scr[...] = s0_ref[0]

    srcs = ((qf_ref, if_ref, ff_ref, of_ref), (qb_ref, ib_ref, fb_ref, ob_ref))
    for d in range(2):
        q_ref, i_ref, f_ref, o_ref = srcs[d]
        for h in range(HG_HEADS):
            cols = slice(h * HG_DK, (h + 1) * HG_DK)
            q = _silu(q_ref[:, cols])
            v = i_ref[:, cols].astype(BF16)
            z = f_ref[:, cols]
            log_lb = lb_ref[2 * d:2 * d + 1, cols]
            log_1m = lb_ref[2 * d + 1:2 * d + 2, cols]
            e = jnp.exp(-jnp.abs(z))
            b_ = log_1m + jnp.minimum(z, 0.0) - jnp.log1p(e)
            logf = jnp.maximum(log_lb, b_) + jnp.log1p(jnp.exp(-jnp.abs(log_lb - b_)))
            k = jnp.exp(log_1m) * jnp.where(z >= 0, e, 1.0) / (1.0 + e)
            l1 = logf.astype(BF16)
            r1 = logf - l1.astype(F32)
            l2 = r1.astype(BF16)
            l3 = (r1 - l2.astype(F32)).astype(BF16)
            a = a_ref[d]
            ex = _dot(a, l1) + _dot(a, l2) + _dot(a, l3)
            l_incl = ex[0:c]
            l_rest = ex[c:2 * c]
            l_tot = l_incl[0:1] + l_rest[0:1]
            st = s_scr[d, h]
            o = _dot_nt((q * jnp.exp(l_incl)).astype(BF16), st.astype(BF16))
            att = m_ref[d, 0] * _dot_nt(q.astype(BF16), k.astype(BF16))
            for lv in range(HG_LEVELS):
                el = jnp.exp(ex[(2 + lv) * c:(3 + lv) * c])
                att = att + m_ref[d, 1 + lv] * _dot_nt((q * el).astype(BF16), (k * el).astype(BF16))
            o = o + _dot(att.astype(BF16), v)
            o_ref[:, cols] = o
            ke = (k * jnp.exp(l_rest)).astype(BF16)
            s_scr[d, h] = st * jnp.exp(l_tot) + _dot_tn(v, ke)

    @pl.when(n == pl.num_programs(1) - 1)
    def _():
        st_ref[0] = s_scr[...]


def hgrn_scan(p, lb_rows, s0_t, a_const, m_const, row0, n_seq, seq_len):
    nt = p.shape[0]
    c = HG_CHUNK
    nchunk = seq_len // c
    base = row0 // c
    w = HG_W
    fwd = lambda col: pl.BlockSpec((c, w), lambda b, n: (base + b * nchunk + n, col))
    bwd = lambda col: pl.BlockSpec((c, w), lambda b, n: (base + b * nchunk + nchunk - 1 - n, col))
    n_rows = n_seq * seq_len
    return pl.pallas_call(
        _hgrn_kernel,
        out_shape=(jax.ShapeDtypeStruct((n_rows, w), F32), jax.ShapeDtypeStruct((n_rows, w), F32),
                   jax.ShapeDtypeStruct((n_seq, 2, HG_HEADS, HG_DV, HG_DK), F32)),
        grid=(n_seq, nchunk),
        in_specs=[fwd(COL_HQ // w), fwd(COL_HI // w), fwd(COL_HFF // w),
                  bwd(COL_HQ // w), bwd(COL_HI // w), bwd(COL_HFB // w),
                  pl.BlockSpec((4, w), lambda b, n: (0, 0)),
                  pl.BlockSpec((1, 2, HG_HEADS, HG_DV, HG_DK), lambda b, n: (b, 0, 0, 0, 0)),
                  pl.BlockSpec(a_const.shape, lambda b, n: (0, 0, 0)),
                  pl.BlockSpec(m_const.shape, lambda b, n: (0, 0, 0, 0))],
        out_specs=(pl.BlockSpec((c, w), lambda b, n: (b * nchunk + n, 0)),
                   pl.BlockSpec((c, w), lambda b, n: (b * nchunk + nchunk - 1 - n, 0)),
                   pl.BlockSpec((1, 2, HG_HEADS, HG_DV, HG_DK), lambda b, n: (b, 0, 0, 0, 0))),
        scratch_shapes=[pltpu.VMEM((2, HG_HEADS, HG_DV, HG_DK), F32)],
        compiler_params=_cparams(("parallel", "arbitrary")),
        name="hgrn_scan",
    )(p, p, p, p, p, p, lb_rows, s0_t, a_const, m_const)


def _hgrn_out_kernel(of_ref, ob_ref, g_ref, gain_ref, o_ref):
    gate = _silu(g_ref[...])
    gain = gain_ref[...]
    for h in range(HG_HEADS):
        cols = slice(h * HG_DV, (h + 1) * HG_DV)
        o = of_ref[:, cols] + ob_ref[:, cols]
        ms = jnp.mean(o * o, axis=-1, keepdims=True)
        o_ref[:, cols] = (o * lax.rsqrt(ms + EPS) * gain * gate[:, cols]).astype(o_ref.dtype)


def hgrn_output(o_f, o_b, p, gain_l, tm=512):
    nt = p.shape[0]
    w = HG_W
    return pl.pallas_call(
        _hgrn_out_kernel,
        out_shape=jax.ShapeDtypeStruct((nt, w), BF16),
        grid=(nt // tm,),
        in_specs=[pl.BlockSpec((tm, w), lambda i: (i, 0)),
                  pl.BlockSpec((tm, w), lambda i: (i, 0)),
                  pl.BlockSpec((tm, w), lambda i: (i, COL_HG // w)),
                  pl.BlockSpec((1, HG_DV), lambda i: (0, 0))],
        out_specs=pl.BlockSpec((tm, w), lambda i: (i, 0)),
        compiler_params=_cparams(("parallel",)),
        name="hgrn_output",
    )(o_f, o_b, p, gain_l)


def _rope_tile_index(i, tm, n_ctx_rows, lat_len):
    r0 = i * tm
    return jnp.where(r0 < n_ctx_rows, 0, 1 + ((r0 - n_ctx_rows) % lat_len) // tm)


def _mla_q_kernel(x_ref, g_ref, w1_ref, w2_ref, g1_ref, g2_ref, cos_ref, sin_ref, o_ref, *, scale):
    x = x_ref[...]
    ms = jnp.mean(x * x, axis=-1, keepdims=True)
    xn = (x * lax.rsqrt(ms + EPS) * g_ref[...]).astype(BF16)
    a = _dot(xn, w1_ref[...])
    b = _dot(xn, w2_ref[...])
    cos = cos_ref[...]
    sin = sin_ref[...]
    for h in range(MLA_HEADS):
        cols = slice(h * QK_SLAB, (h + 1) * QK_SLAB)
        ah = a[:, cols]
        r = lax.rsqrt(jnp.sum(ah * ah, axis=-1, keepdims=True) * (1.0 / MLA_QK) + EPS) * scale
        t = ah * g1_ref[...] * cos + b[:, cols] * g2_ref[...] * sin
        o_ref[:, cols] = (t * r).astype(o_ref.dtype)


def mla_queries(p, q_norm_g, w1, w2, g1, g2, cos_t, sin_t, n_ctx_rows, lat_len, tm=256):
    nt = p.shape[0]
    r = MLA_Q_RANK
    n = MLA_HEADS * QK_SLAB
    tidx = lambda i: (_rope_tile_index(i, tm, n_ctx_rows, lat_len), 0)
    kern = functools.partial(_mla_q_kernel, scale=MLA_QK ** -0.5)
    return pl.pallas_call(
        kern,
        out_shape=jax.ShapeDtypeStruct((nt, n), BF16),
        grid=(nt // tm,),
        in_specs=[pl.BlockSpec((tm, r), lambda i: (i, COL_MQ // r)),
                  pl.BlockSpec((1, r), lambda i: (0, 0)),
                  pl.BlockSpec((r, n), lambda i: (0, 0)),
                  pl.BlockSpec((r, n), lambda i: (0, 0)),
                  pl.BlockSpec((1, QK_SLAB), lambda i: (0, 0)),
                  pl.BlockSpec((1, QK_SLAB), lambda i: (0, 0)),
                  pl.BlockSpec((tm, QK_SLAB), tidx),
                  pl.BlockSpec((tm, QK_SLAB), tidx)],
        out_specs=pl.BlockSpec((tm, n), lambda i: (i, 0)),
        compiler_params=_cparams(("parallel",)),
        name="mla_queries",
    )(p, q_norm_g, w1, w2, g1, g2, cos_t, sin_t)


def _mla_kv_kernel(x_ref, kr_ref, g_ref, wk_ref, wv_ref, gn_ref, gr_ref, g2_ref, cos_ref, sin_ref,
                   ckv_ref, k_ref, v_ref, *, normalize):
    x = x_ref[...]
    if normalize:
        ms = jnp.mean(x * x, axis=-1, keepdims=True)
        x = x * lax.rsqrt(ms + EPS) * g_ref[...]
    ckv_ref[...] = x
    xb = x.astype(BF16)
    kn = _dot(xb, wk_ref[...])
    v_ref[...] = _dot(xb, wv_ref[...]).astype(v_ref.dtype)
    kr = kr_ref[:, 0:128]
    krp = kr_ref[:, 128:256]
    ssr = jnp.sum(kr * kr, axis=-1, keepdims=True)
    rt = kr * gr_ref[...] * cos_ref[...] + krp * g2_ref[...] * sin_ref[...]
    for h in range(MLA_HEADS):
        ah = kn[:, h * MLA_NOPE:(h + 1) * MLA_NOPE]
        r = lax.rsqrt((jnp.sum(ah * ah, axis=-1, keepdims=True) + ssr) * (1.0 / MLA_QK) + EPS)
        k_ref[:, h * QK_SLAB:h * QK_SLAB + MLA_NOPE] = (ah * gn_ref[...] * r).astype(k_ref.dtype)
        k_ref[:, h * QK_SLAB + MLA_NOPE:(h + 1) * QK_SLAB] = (rt * r).astype(k_ref.dtype)


def mla_keys_values(x, x_col, kr, kv_norm_g, wk, wv, gn, gr, g2, cos_t, sin_t, normalize,
                    n_ctx_rows, lat_len, tm=256):
    nt = x.shape[0]
    r = MLA_KV_RANK
    tidx = lambda i: (_rope_tile_index(i, tm, n_ctx_rows, lat_len), 0)
    kern = functools.partial(_mla_kv_kernel, normalize=normalize)
    return pl.pallas_call(
        kern,
        out_shape=(jax.ShapeDtypeStruct((nt, r), F32),
                   jax.ShapeDtypeStruct((nt, MLA_HEADS * QK_SLAB), BF16),
                   jax.ShapeDtypeStruct((nt, MLA_HEADS * MLA_V), BF16)),
        grid=(nt // tm,),
        in_specs=[pl.BlockSpec((tm, r), lambda i: (i, x_col)),
                  pl.BlockSpec((tm, KR_COLS), lambda i: (i, 0)),
                  pl.BlockSpec((1, r), lambda i: (0, 0)),
                  pl.BlockSpec(wk.shape, lambda i: (0, 0)),
                  pl.BlockSpec(wv.shape, lambda i: (0, 0)),
                  pl.BlockSpec((1, 128), lambda i: (0, 0)),
                  pl.BlockSpec((1, 128), lambda i: (0, 0)),
                  pl.BlockSpec((1, 128), lambda i: (0, 0)),
                  pl.BlockSpec((tm, 128), tidx),
                  pl.BlockSpec((tm, 128), tidx)],
        out_specs=(pl.BlockSpec((tm, r), lambda i: (i, 0)),
                   pl.BlockSpec((tm, MLA_HEADS * QK_SLAB), lambda i: (i, 0)),
                   pl.BlockSpec((tm, MLA_HEADS * MLA_V), lambda i: (i, 0))),
        compiler_params=_cparams(("parallel",)),
        name="mla_keys_values",
    )(x, kr, kv_norm_g, wk, wv, gn, gr, g2, cos_t, sin_t)


def _attn_kernel(*refs, tk, n_src):
    q_ref = refs[0]
    kv_refs = refs[1:1 + 2 * n_src]
    o_ref = refs[1 + 2 * n_src]
    q = q_ref[...]
    tq = q.shape[0]
    m = jnp.full((tq, 1), -jnp.inf, F32)
    l = jnp.zeros((tq, 1), F32)
    acc = jnp.zeros((tq, MLA_V), F32)
    for s in range(n_src):
        k_ref, v_ref = kv_refs[2 * s], kv_refs[2 * s + 1]

        def body(c, carry, k_ref=k_ref, v_ref=v_ref):
            m, l, acc = carry
            rows = pl.ds(pl.multiple_of(c * tk, tk), tk)
            sc = _dot_nt(q, k_ref[rows, :])
            m_new = jnp.maximum(m, jnp.max(sc, axis=-1, keepdims=True))
            pr = jnp.exp(sc - m_new)
            alpha = jnp.exp(m - m_new)
            l = alpha * l + jnp.sum(pr, axis=-1, keepdims=True)
            acc = alpha * acc + _dot(pr.astype(BF16), v_ref[rows, :])
            return m_new, l, acc

        m, l, acc = lax.fori_loop(0, k_ref.shape[0] // tk, body, (m, l, acc))
    o_ref[...] = (acc / l).astype(o_ref.dtype)


def mla_attention(q, q_row0, n_seq, seq_len, sources, tq, tk):
    nq = seq_len // tq
    qb0 = q_row0 // tq
    in_specs = [pl.BlockSpec((tq, QK_SLAB), lambda b, h, i: (qb0 + b * nq + i, h))]
    args = [q]
    for k_arr, v_arr, row0, rows in sources:
        kb0 = row0 // rows
        in_specs.append(pl.BlockSpec((rows, QK_SLAB), lambda b, h, i, kb0=kb0: (kb0 + b, h)))
        in_specs.append(pl.BlockSpec((rows, MLA_V), lambda b, h, i, kb0=kb0: (kb0 + b, h)))
        args += [k_arr, v_arr]
    kern = functools.partial(_attn_kernel, tk=tk, n_src=len(sources))
    return pl.pallas_call(
        kern,
        out_shape=jax.ShapeDtypeStruct((n_seq * seq_len, MLA_HEADS * MLA_V), BF16),
        grid=(n_seq, MLA_HEADS, nq),
        in_specs=in_specs,
        out_specs=pl.BlockSpec((tq, MLA_V), lambda b, h, i: (b * nq + i, h)),
        compiler_params=_cparams(("parallel", "parallel", "arbitrary")),
        name="mla_attention",
    )(*args)


def _merge_kernel(ya_ref, yb_ref, yc_ref, yd_ref, wb_ref, g0_ref, g1_ref, g2_ref, g3_ref, o_ref):
    acc = _sigmoid(g0_ref[...]) * _dot(ya_ref[...], wb_ref[0:512, :])
    acc += _sigmoid(g1_ref[...]) * _dot(yb_ref[...], wb_ref[512:1024, :])
    acc += _sigmoid(g2_ref[...]) * _dot(yc_ref[...], wb_ref[1024:1536, :])
    acc += _sigmoid(g3_ref[...]) * _dot(yd_ref[...], wb_ref[1536:2560, :])
    o_ref[...] = acc.astype(o_ref.dtype)


def branch_merge(ya, yb, yc, yd, wb, p, tm=512, tn=512):
    nt = ya.shape[0]
    d = wb.shape[1]
    gate = lambda b: pl.BlockSpec((tm, tn), lambda i, j, b=b: (i, (GATE_COL0 + b * d) // tn + j))
    return pl.pallas_call(
        _merge_kernel,
        out_shape=jax.ShapeDtypeStruct((nt, d), BF16),
        grid=(nt // tm, d // tn),
        in_specs=[pl.BlockSpec((tm, 512), lambda i, j: (i, 0)),
                  pl.BlockSpec((tm, 512), lambda i, j: (i, 0)),
                  pl.BlockSpec((tm, 512), lambda i, j: (i, 0)),
                  pl.BlockSpec((tm, 1024), lambda i, j: (i, 0)),
                  pl.BlockSpec((MIX_W, tn), lambda i, j: (0, j)),
                  gate(0), gate(1), gate(2), gate(3)],
        out_specs=pl.BlockSpec((tm, tn), lambda i, j: (i, j)),
        compiler_params=_cparams(("parallel", "arbitrary")),
        name="branch_merge",
    )(ya, yb, yc, yd, wb, p, p, p, p)


def _resid_kernel(a_ref, w_ref, x_ref, mod_ref, o_ref, *, mod_row):
    g = mod_ref[0][mod_row:mod_row + 1]
    o_ref[...] = x_ref[...] + g * _dot(a_ref[...], w_ref[...])


def _resid_w_kernel(a_ref, w_ref, x_ref, mod_ref, cw_ref, o_ref, *, mod_row, col):
    g = mod_ref[0][mod_row:mod_row + 1]
    o_ref[...] = x_ref[...] + (g * cw_ref[:, col:col + 1]) * _dot(a_ref[...], w_ref[...])


def residual_matmul(a, w, x, mod_l, mod_row, tm, tn, row_weight=None, col=0):
    nt, k = a.shape
    d = w.shape[1]
    in_specs = [pl.BlockSpec((tm, k), lambda i, j: (i, 0)),
                pl.BlockSpec((k, tn), lambda i, j: (0, j)),
                pl.BlockSpec((tm, tn), lambda i, j: (i, j)),
                pl.BlockSpec((1, 6, tn), lambda i, j: (_group_of(i, tm), 0, j))]
    args = [a, w, x, mod_l]
    if row_weight is None:
        kern = functools.partial(_resid_kernel, mod_row=mod_row)
    else:
        kern = functools.partial(_resid_w_kernel, mod_row=mod_row, col=col)
        in_specs.append(pl.BlockSpec((tm, 128), lambda i, j: (i, 0)))
        args.append(row_weight)
    return pl.pallas_call(
        kern,
        out_shape=jax.ShapeDtypeStruct((nt, d), F32),
        grid=(nt // tm, d // tn),
        in_specs=in_specs,
        out_specs=pl.BlockSpec((tm, tn), lambda i, j: (i, j)),
        input_output_aliases={2: 0},
        compiler_params=_cparams(("parallel", "arbitrary")),
        name="residual_matmul",
    )(*args)


def _ffn_up_kernel(x_ref, mod_ref, gain_ref, wg_ref, wu_ref, o_ref, h_scr):
    @pl.when(pl.program_id(1) == 0)
    def _():
        m = mod_ref[0]
        h_scr[...] = _mod_norm(x_ref[...], gain_ref[...], m[3:4], m[4:5]).astype(BF16)

    h = h_scr[...]
    o_ref[...] = (_silu(_dot(h, wg_ref[...])) * _dot(h, wu_ref[...])).astype(o_ref.dtype)


def ffn_up(x, mod_l, gain, wg, wu, tm=1024, tn=512):
    nt, d = x.shape
    f = wg.shape[1]
    return pl.pallas_call(
        _ffn_up_kernel,
        out_shape=jax.ShapeDtypeStruct((nt, f), BF16),
        grid=(nt // tm, f // tn),
        in_specs=[pl.BlockSpec((tm, d), lambda i, j: (i, 0)),
                  pl.BlockSpec((1, 6, d), lambda i, j: (_group_of(i, tm), 0, 0)),
                  pl.BlockSpec((1, d), lambda i, j: (0, 0)),
                  pl.BlockSpec((d, tn), lambda i, j: (0, j)),
                  pl.BlockSpec((d, tn), lambda i, j: (0, j))],
        out_specs=pl.BlockSpec((tm, tn), lambda i, j: (i, j)),
        scratch_shapes=[pltpu.VMEM((tm, d), BF16)],
        compiler_params=_cparams(("parallel", "arbitrary")),
        name="ffn_up",
    )(x, mod_l, gain, wg, wu)


def _expert_up_kernel(h_ref, wg_ref, wu_ref, o_ref):
    h = h_ref[...]
    o_ref[...] = (_silu(_dot(h, wg_ref[...])) * _dot(h, wu_ref[...])).astype(o_ref.dtype)


def expert_up(h, wg, wu, tm=1024, tn=512):
    nt, d = h.shape
    f = wg.shape[1]
    return pl.pallas_call(
        _expert_up_kernel,
        out_shape=jax.ShapeDtypeStruct((nt, f), BF16),
        grid=(nt // tm, f // tn),
        in_specs=[pl.BlockSpec((tm, d), lambda i, j: (i, 0)),
                  pl.BlockSpec((d, tn), lambda i, j: (0, j)),
                  pl.BlockSpec((d, tn), lambda i, j: (0, j))],
        out_specs=pl.BlockSpec((tm, tn), lambda i, j: (i, j)),
        compiler_params=_cparams(("parallel", "arbitrary")),
        name="expert_up",
    )(h, wg, wu)


def _router_kernel(x_ref, mod_ref, gain_ref, wr_ref, br_ref, h_ref, cw_ref):
    m = mod_ref[0]
    h = _mod_norm(x_ref[...], gain_ref[...], m[3:4], m[4:5])
    h_ref[...] = h.astype(h_ref.dtype)
    logits = jnp.dot(h, wr_ref[...], preferred_element_type=F32, precision=lax.Precision.HIGHEST) + br_ref[...]
    lane = lax.broadcasted_iota(jnp.int32, logits.shape, 1).astype(F32)
    neg = jnp.float32(-jnp.inf)
    logits = jnp.where(lane < N_EXPERTS, logits, neg)
    m1 = jnp.max(logits, axis=-1, keepdims=True)
    i1 = jnp.min(jnp.where(logits == m1, lane, 128.0), axis=-1, keepdims=True)
    rest = jnp.where(lane == i1, neg, logits)
    m2 = jnp.max(rest, axis=-1, keepdims=True)
    i2 = jnp.min(jnp.where(rest == m2, lane, 128.0), axis=-1, keepdims=True)
    e2 = jnp.exp(m2 - m1)
    w1 = 1.0 / (1.0 + e2)
    w2 = e2 / (1.0 + e2)
    cw_ref[...] = jnp.where(lane == i1, w1, 0.0) + jnp.where(lane == i2, w2, 0.0)


def moe_router(x, mod_l, gain, w_router_pad, b_router_pad, tm=256):
    nt, d = x.shape
    return pl.pallas_call(
        _router_kernel,
        out_shape=(jax.ShapeDtypeStruct((nt, d), BF16), jax.ShapeDtypeStruct((nt, 128), F32)),
        grid=(nt // tm,),
        in_specs=[pl.BlockSpec((tm, d), lambda i: (i, 0)),
                  pl.BlockSpec((1, 6, d), lambda i: (_group_of(i, tm), 0, 0)),
                  pl.BlockSpec((1, d), lambda i: (0, 0)),
                  pl.BlockSpec((d, 128), lambda i: (0, 0)),
                  pl.BlockSpec((1, 128), lambda i: (0, 0))],
        out_specs=(pl.BlockSpec((tm, d), lambda i: (i, 0)), pl.BlockSpec((tm, 128), lambda i: (i, 0))),
        compiler_params=_cparams(("parallel",)),
        name="moe_router",
    )(x, mod_l, gain, w_router_pad, b_router_pad)


def _rope_partner_perm():
    q = MLA_ROPE // 4
    perm = np.concatenate([np.arange(q, 2 * q), np.arange(0, q), np.arange(3 * q, 4 * q), np.arange(2 * q, 3 * q)])
    sign = np.concatenate([-np.ones(q), np.ones(q), -np.ones(q), np.ones(q)]).astype(np.float32)
    return perm, sign


def _rope_tables(lat_len, tm):
    half = MLA_ROPE // 2
    pos = jnp.arange(lat_len)
    inv = ROPE_BASE ** (-jnp.arange(0, half, 2, dtype=F32) / half)
    ang_r = (pos // GRID_W).astype(F32)[:, None] * inv[None, :]
    ang_c = (pos % GRID_W).astype(F32)[:, None] * inv[None, :]
    ang = jnp.concatenate([ang_r, ang_r, ang_c, ang_c], axis=-1)
    cos = jnp.concatenate([jnp.ones((tm, MLA_ROPE), F32), jnp.cos(ang)], axis=0)
    sin = jnp.concatenate([jnp.zeros((tm, MLA_ROPE), F32), jnp.sin(ang)], axis=0)
    return cos, sin


def _pad_lanes(a, left, total, fill=0.0):
    return jnp.pad(a, ((0, 0), (left, total - left - a.shape[1])), constant_values=fill)


def kernel(x_prompt, x_sample, c, cache_mla_ckv, cache_mla_krope, state_hgrn, c_ctx, norm1, norm2, w_mod, b_mod, w_in, conv_w, sg_w, sg_b, hg_lb, hg_norm, mla_q_norm, mla_kv_norm, w_uq, w_ukv, qk_norm_q, qk_norm_k, w_branch, w_o, w_ff_gate, w_ff_up, w_ff_down, w_router, b_router, w_moe_gate, w_moe_up, w_moe_down):
    batch, seq, d = x_prompt.shape
    dec_batch, dec_seq, _ = x_sample.shape
    past = cache_mla_ckv.shape[2]
    n_ctx = batch * seq
    n_lat = dec_batch * dec_seq
    assert n_ctx == MOD_GROUP_ROWS and dec_seq == MOD_GROUP_ROWS and dec_batch + 1 <= 8

    x = jnp.concatenate([x_prompt.reshape(n_ctx, d), x_sample.reshape(n_lat, d)], axis=0)
    cv = jnp.concatenate([c_ctx[None, :], c, jnp.zeros((8 - 1 - dec_batch, d), F32)], axis=0)
    mod = modulation_all(cv, w_mod, b_mod).reshape(DEPTH, 8, 6, d)

    lb_cum = jnp.cumsum(jax.nn.softmax(hg_lb.astype(F32), axis=1), axis=1)
    lb_all = lb_cum - lb_cum[:, :1]
    a_np, m_np = _hgrn_constants()
    a_const = jnp.asarray(a_np, BF16)
    m_const = jnp.asarray(m_np, F32)
    zero_state = jnp.zeros((batch, 2, HG_HEADS, HG_DV, HG_DK), F32)
    state_t = jnp.swapaxes(state_hgrn, -1, -2)

    perm, sign = _rope_partner_perm()
    rope_tm = 256
    cos64, sin64 = _rope_tables(dec_seq, rope_tm)
    cos_q = jnp.concatenate([jnp.ones((cos64.shape[0], MLA_NOPE), F32), cos64, jnp.zeros_like(cos64)], axis=1)
    sin_q = jnp.concatenate([jnp.zeros((sin64.shape[0], MLA_NOPE), F32), sin64, jnp.zeros_like(sin64)], axis=1)
    cos_k = _pad_lanes(cos64, 0, 128)
    sin_k = _pad_lanes(sin64, 0, 128)
    cos_id = jnp.ones((rope_tm, 128), F32)
    sin_id = jnp.zeros((rope_tm, 128), F32)

    ckv_out, kr_out, st_out = [], [], []
    for l in range(DEPTH):
        mod_l = mod[l]
        wl = w_in[l]
        w_main = jnp.concatenate([wl[:, :MIXER_COLS], wl[:, MIXER_COLS + MLA_ROPE:]], axis=1).astype(BF16)
        wkr = wl[:, MIXER_COLS:MIXER_COLS + MLA_ROPE]
        zpad = jnp.zeros((d, 64), F32)
        w_kr = jnp.concatenate([wkr, zpad, wkr[:, perm] * sign[None, :], zpad], axis=1).astype(BF16)

        p, kr = input_projection(x, mod_l, norm1[l][None, :], w_main, w_kr)

        ya = short_conv_mixer(p, conv_w[l], n_ctx, seq, dec_seq)
        yb = spatial_gate_mixer(p, sg_w[l], sg_b[l])

        lb = jnp.stack([lb_all[0, l], lb_all[1, l]])
        lb_rows = jnp.stack([jnp.log(lb[0]), jnp.log1p(-lb[0]), jnp.log(lb[1]), jnp.log1p(-lb[1])])
        of_c, ob_c, st_c = hgrn_scan(p, lb_rows, zero_state, a_const, m_const, 0, batch, seq)
        of_l, ob_l, _ = hgrn_scan(p, lb_rows, state_t[:, l], a_const, m_const, n_ctx, dec_batch, dec_seq)
        o_f = jnp.concatenate([of_c, of_l], axis=0)
        o_b = jnp.concatenate([ob_c, ob_l], axis=0)
        yc = hgrn_output(o_f, o_b, p, hg_norm[l][None, :])
        st_out.append(jnp.swapaxes(st_c, -1, -2))

        wq = w_uq[l].reshape(MLA_Q_RANK, MLA_HEADS, MLA_QK)
        wq_rope = wq[:, :, MLA_NOPE:]
        zq = jnp.zeros((MLA_Q_RANK, MLA_HEADS, QK_SLAB - MLA_QK), F32)
        w1 = jnp.concatenate([wq, zq], axis=2).reshape(MLA_Q_RANK, -1).astype(BF16)
        w2 = jnp.concatenate([jnp.zeros_like(wq[:, :, :MLA_NOPE]), wq_rope[:, :, perm] * sign[None, None, :], zq],
                             axis=2).reshape(MLA_Q_RANK, -1).astype(BF16)
        gq = qk_norm_q[l]
        g1 = _pad_lanes(gq[None, :], 0, QK_SLAB)
        g2q = _pad_lanes(gq[MLA_NOPE:][perm][None, :], MLA_NOPE, QK_SLAB)
        q_all = mla_queries(p, mla_q_norm[l][None, :], w1, w2, g1, g2q, cos_q, sin_q, n_ctx, dec_seq, tm=rope_tm)

        wkv = w_ukv[l].reshape(MLA_KV_RANK, MLA_HEADS, MLA_NOPE + MLA_V)
        wk = wkv[:, :, :MLA_NOPE].reshape(MLA_KV_RANK, -1).astype(BF16)
        wv = wkv[:, :, MLA_NOPE:].reshape(MLA_KV_RANK, -1).astype(BF16)
        gk = qk_norm_k[l]
        gn = gk[None, :MLA_NOPE]
        gr = _pad_lanes(gk[None, MLA_NOPE:], 0, 128)
        g2k = _pad_lanes(gk[MLA_NOPE:][perm][None, :], 0, 128)
        ckv, k_all, v_all = mla_keys_values(p, COL_MKV // MLA_KV_RANK, kr, mla_kv_norm[l][None, :], wk, wv,
                                            gn, gr, g2k, cos_k, sin_k, True, n_ctx, dec_seq, tm=rope_tm)
        cache_x = cache_mla_ckv[:, l].reshape(dec_batch * past, MLA_KV_RANK)
        cache_kr = _pad_lanes(cache_mla_krope[:, l].reshape(dec_batch * past, MLA_ROPE), 0, KR_COLS)
        _, k_cache, v_cache = mla_keys_values(cache_x, 0, cache_kr, mla_kv_norm[l][None, :], wk, wv,
                                              gn, gr, g2k, cos_id, sin_id, False, dec_batch * past, dec_seq,
                                              tm=rope_tm)
        yd_c = mla_attention(q_all, 0, batch, seq, [(k_all, v_all, 0, seq)], tq=seq, tk=seq)
        yd_l = mla_attention(q_all, n_ctx, dec_batch, dec_seq,
                             [(k_all, v_all, n_ctx, dec_seq), (k_cache, v_cache, 0, past)], tq=512, tk=512)
        yd = jnp.concatenate([yd_c, yd_l], axis=0)
        ckv_out.append(ckv[:n_ctx].reshape(batch, seq, MLA_KV_RANK))
        kr_out.append(kr[:n_ctx, :MLA_ROPE].reshape(batch, seq, MLA_ROPE))

        merged = branch_merge(ya, yb, yc, yd, w_branch[l].astype(BF16), p)
        x = residual_matmul(merged, w_o[l].astype(BF16), x, mod_l, 2, tm=1024, tn=512)

        j = l // 2
        if l % 2 == 0:
            act = ffn_up(x, mod_l, norm2[l][None, :], w_ff_gate[j].astype(BF16), w_ff_up[j].astype(BF16))
            x = residual_matmul(act, w_ff_down[j].astype(BF16), x, mod_l, 5, tm=1024, tn=256)
        else:
            wr = _pad_lanes(w_router[j], 0, 128)
            br = _pad_lanes(b_router[j][None, :], 0, 128)
            h2, cw = moe_router(x, mod_l, norm2[l][None, :], wr, br)
            for e in range(N_EXPERTS):
                act = expert_up(h2, w_moe_gate[j, e].astype(BF16), w_moe_up[j, e].astype(BF16))
                x = residual_matmul(act, w_moe_down[j, e].astype(BF16), x, mod_l, 5, tm=1024, tn=256,
                                    row_weight=cw, col=e)

    y_prompt = x[:n_ctx].reshape(batch, seq, d)
    y_sample = x[n_ctx:].reshape(dec_batch, dec_seq, d)
    return (y_prompt, y_sample, jnp.stack(ckv_out, axis=1), jnp.stack(kr_out, axis=1), jnp.stack(st_out, axis=1))
```
